```python
import jax, jax.numpy as jnp
from jax import lax
import numpy as np

D_MODEL = 4096
BATCH = 2
SEQ = 4096
DEPTH = 4

MEM_LEN = 256
CHUNK = 128
MIX_WIDTH = D_MODEL
SGU_WIDTH = MIX_WIDTH // 2
SGU_HEAD_DIM = 128
SGU_HEADS = SGU_WIDTH // SGU_HEAD_DIM
RET_WIDTH = MIX_WIDTH - SGU_WIDTH
RET_HEAD_DIM = 256
RET_HEADS = RET_WIDTH // RET_HEAD_DIM
IN_COLS = 2 * SGU_WIDTH + 4 * RET_WIDTH
XATTN_HEADS = 4
XATTN_HEAD_DIM = 128
XATTN_WIDTH = XATTN_HEADS * XATTN_HEAD_DIM
D_FF_DENSE = ((8 * D_MODEL // 3 + 255) // 256) * 256
N_EXPERTS = 8
TOP_K = 2
D_FF_EXPERT = 7 * D_MODEL // 16
ROPE_BASE = 10000.0
EPS = 1e-6
MAX_POS_OFFSET = 1024

kernel_name = "hybrid_sgu_retention_memory_moe"


def _rmsnorm(x, gain):
    xf = x.astype(jnp.float32)
    inv = lax.rsqrt(jnp.mean(xf * xf, axis=-1, keepdims=True) + EPS)
    return (xf * inv).astype(x.dtype) * gain


def _rotary(t, positions):
    half = t.shape[-1] // 2
    inv_freq = ROPE_BASE ** (-jnp.arange(half, dtype=jnp.float32) / half)
    ang = positions.astype(jnp.float32)[..., None] * inv_freq
    cos = jnp.cos(ang)[:, :, None, :].astype(t.dtype)
    sin = jnp.sin(ang)[:, :, None, :].astype(t.dtype)
    t1, t2 = t[..., :half], t[..., half:]
    return jnp.concatenate([t1 * cos - t2 * sin, t2 * cos + t1 * sin], axis=-1)


def _spatial_gating(u, v, sgu_norm, w_spatial, b_spatial):
    B, S, _ = u.shape
    nc = S // CHUNK
    u = jax.nn.gelu(u)
    v = _rmsnorm(jax.nn.gelu(v), sgu_norm)
    v = v.reshape(B, nc, CHUNK, SGU_HEADS, SGU_HEAD_DIM)
    causal = jnp.tril(jnp.ones((CHUNK, CHUNK), dtype=bool))
    w = jnp.where(causal[None], w_spatial, 0.0).astype(v.dtype)
    mixed = jnp.einsum('hij,bcjhd->bcihd', w, v) + b_spatial.T[None, None, :, :, None]
    return u * mixed.reshape(B, S, SGU_WIDTH)


def _retention(q, k, v, g, positions, ret_norm):
    B, S, _ = q.shape
    nc = S // CHUNK
    dt = q.dtype
    q = _rotary(q.reshape(B, S, RET_HEADS, RET_HEAD_DIM), positions)
    k = _rotary(k.reshape(B, S, RET_HEADS, RET_HEAD_DIM), positions) * (RET_HEAD_DIM ** -0.5)
    v = v.reshape(B, S, RET_HEADS, RET_HEAD_DIM)
    log_gamma = jnp.log1p(-jnp.exp2(-5.0 - jnp.arange(RET_HEADS, dtype=jnp.float32)))
    idx = jnp.arange(CHUNK, dtype=jnp.float32)
    rel = idx[:, None] - idx[None, :]
    decay_intra = jnp.where(rel[None] >= 0, jnp.exp(log_gamma[:, None, None] * jnp.maximum(rel, 0.0)[None]), 0.0).astype(dt)
    decay_key = jnp.exp(log_gamma[:, None] * (CHUNK - 1 - idx)[None]).astype(dt)
    decay_query = jnp.exp(log_gamma[:, None] * (idx + 1)[None]).astype(dt)
    decay_chunk = jnp.exp(log_gamma * CHUNK)

    def to_chunks(t):
        return t.reshape(B, nc, CHUNK, RET_HEADS, RET_HEAD_DIM).transpose(1, 0, 3, 2, 4)

    def step(state, qkv):
        qc, kc, vc = qkv
        scores = jnp.einsum('bhid,bhjd->bhij', qc, kc) * decay_intra[None]
        intra = jnp.einsum('bhij,bhjd->bhid', scores, vc)
        inter = jnp.einsum('bhid,bhde->bhie', qc, state.astype(dt)) * decay_query[None, :, :, None]
        update = jnp.einsum('bhjd,bhje->bhde', kc * decay_key[None, :, :, None], vc)
        state = decay_chunk[None, :, None, None] * state + update.astype(jnp.float32)
        return state, intra + inter

    state0 = jnp.zeros((B, RET_HEADS, RET_HEAD_DIM, RET_HEAD_DIM), dtype=jnp.float32)
    _, out = lax.scan(step, state0, (to_chunks(q), to_chunks(k), to_chunks(v)))
    out = out.transpose(1, 0, 3, 2, 4).reshape(B, S, RET_HEADS, RET_HEAD_DIM)
    out = _rmsnorm(out, ret_norm.reshape(RET_HEADS, RET_HEAD_DIM))
    return jax.nn.silu(g) * out.reshape(B, S, RET_WIDTH)


def _memory_attention(xn, mem_n, w_cq, w_ck, w_cv, w_co):
    B, S, _ = xn.shape
    M = mem_n.shape[1]
    q = (xn @ w_cq).reshape(B, S, XATTN_HEADS, XATTN_HEAD_DIM)
    k = (mem_n @ w_ck).reshape(B, M, XATTN_HEADS, XATTN_HEAD_DIM)
    v = (mem_n @ w_cv).reshape(B, M, XATTN_HEADS, XATTN_HEAD_DIM)
    s = jnp.einsum('bshd,bmhd->bhsm', q, k).astype(jnp.float32) * (XATTN_HEAD_DIM ** -0.5)
    p = jax.nn.softmax(s, axis=-1).astype(v.dtype)
    o = jnp.einsum('bhsm,bmhd->bshd', p, v).reshape(B, S, XATTN_WIDTH)
    return o @ w_co


def _swiglu(h, w_gate, w_up, w_down):
    return (jax.nn.silu(h @ w_gate) * (h @ w_up)) @ w_down


def _moe(h, w_router, we_gate, we_up, we_down):
    B, S, D = h.shape
    t = h.reshape(B * S, D)
    logits = (t @ w_router).astype(jnp.float32)
    top_vals, top_idx = lax.top_k(logits, TOP_K)
    top_w = jax.nn.softmax(top_vals, axis=-1)
    combine = jnp.sum(jax.nn.one_hot(top_idx, N_EXPERTS, dtype=jnp.float32) * top_w[..., None], axis=1).astype(t.dtype)
    y = jnp.zeros_like(t)
    for e in range(N_EXPERTS):
        y = y + combine[:, e:e + 1] * _swiglu(t, we_gate[e], we_up[e], we_down[e])
    return y.reshape(B, S, D)


def setup_inputs(seed: int = 0) -> dict:
    key = jax.random.key(seed)
    ks = jax.random.split(key, 26)
    n_dense = (DEPTH + 1) // 2
    n_moe = DEPTH // 2

    def normal(k, shape, fan_in):
        return jax.random.normal(k, shape, jnp.float32) * (fan_in ** -0.5)

    def gain(k, shape):
        return 1.0 + 0.02 * jax.random.normal(k, shape, jnp.float32)

    x = jax.random.normal(ks[0], (BATCH, SEQ, D_MODEL), jnp.float32)
    mem = jax.random.normal(ks[1], (BATCH, MEM_LEN, D_MODEL), jnp.float32)
    offset = jax.random.randint(ks[2], (BATCH, 1), 0, MAX_POS_OFFSET, dtype=jnp.int32)
    positions = offset + jnp.arange(SEQ, dtype=jnp.int32)[None, :]
    return {
        "x": x,
        "mem": mem,
        "positions": positions,
        "mix_norm": gain(ks[3], (DEPTH, D_MODEL)),
        "w_in": normal(ks[4], (DEPTH, D_MODEL, IN_COLS), D_MODEL),
        "sgu_norm": gain(ks[5], (DEPTH, SGU_WIDTH)),
        "w_spatial": normal(ks[6], (DEPTH, SGU_HEADS, CHUNK, CHUNK), CHUNK),
        "b_spatial": gain(ks[7], (DEPTH, SGU_HEADS, CHUNK)),
        "ret_norm": gain(ks[8], (DEPTH, RET_WIDTH)),
        "w_out": normal(ks[9], (DEPTH, MIX_WIDTH, D_MODEL), MIX_WIDTH),
        "xattn_norm": gain(ks[10], (DEPTH, D_MODEL)),
        "mem_norm": gain(ks[11], (DEPTH, D_MODEL)),
        "w_cq": normal(ks[12], (DEPTH, D_MODEL, XATTN_WIDTH), D_MODEL),
        "w_ck": normal(ks[13], (DEPTH, D_MODEL, XATTN_WIDTH), D_MODEL),
        "w_cv": normal(ks[14], (DEPTH, D_MODEL, XATTN_WIDTH), D_MODEL),
        "w_co": normal(ks[15], (DEPTH, XATTN_WIDTH, D_MODEL), XATTN_WIDTH),
        "ffn_norm": gain(ks[16], (DEPTH, D_MODEL)),
        "w_gate": normal(ks[17], (n_dense, D_MODEL, D_FF_DENSE), D_MODEL),
        "w_up": normal(ks[18], (n_dense, D_MODEL, D_FF_DENSE), D_MODEL),
        "w_down": normal(ks[19], (n_dense, D_FF_DENSE, D_MODEL), D_FF_DENSE),
        "w_router": normal(ks[20], (n_moe, D_MODEL, N_EXPERTS), D_MODEL),
        "we_gate": normal(ks[21], (n_moe, N_EXPERTS, D_MODEL, D_FF_EXPERT), D_MODEL),
        "we_up": normal(ks[22], (n_moe, N_EXPERTS, D_MODEL, D_FF_EXPERT), D_MODEL),
        "we_down": normal(ks[23], (n_moe, N_EXPERTS, D_FF_EXPERT, D_MODEL), D_FF_EXPERT),
        "final_norm": gain(ks[24], (D_MODEL,)),
    }


def reference(x, mem, positions, mix_norm, w_in, sgu_norm, w_spatial, b_spatial, ret_norm, w_out,
              xattn_norm, mem_norm, w_cq, w_ck, w_cv, w_co, ffn_norm, w_gate, w_up, w_down,
              w_router, we_gate, we_up, we_down, final_norm):
    splits = [SGU_WIDTH, 2 * SGU_WIDTH, 2 * SGU_WIDTH + RET_WIDTH,
              2 * SGU_WIDTH + 2 * RET_WIDTH, 2 * SGU_WIDTH + 3 * RET_WIDTH]
    for layer in range(DEPTH):
        h = _rmsnorm(x, mix_norm[layer])
        proj = h @ w_in[layer]
        u, v, q, k, vr, g = jnp.split(proj, splits, axis=-1)
        sgu_out = _spatial_gating(u, v, sgu_norm[layer], w_spatial[layer], b_spatial[layer])
        ret_out = _retention(q, k, vr, g, positions, ret_norm[layer])
        x = x + jnp.concatenate([sgu_out, ret_out], axis=-1) @ w_out[layer]
        x = x + _memory_attention(_rmsnorm(x, xattn_norm[layer]), _rmsnorm(mem, mem_norm[layer]),
                                  w_cq[layer], w_ck[layer], w_cv[layer], w_co[layer])
        h = _rmsnorm(x, ffn_norm[layer])
        if layer % 2 == 0:
            x = x + _swiglu(h, w_gate[layer // 2], w_up[layer // 2], w_down[layer // 2])
        else:
            x = x + _moe(h, w_router[layer // 2], we_gate[layer // 2], we_up[layer // 2], we_down[layer // 2])
    return _rmsnorm(x, final_norm)
```

```python
import functools
import math

import jax
import jax.numpy as jnp
from jax import lax
from jax.experimental import pallas as pl
from jax.experimental.pallas import tpu as pltpu

CHUNK = 128
SGU_HEAD_DIM = 128
RET_HEAD_DIM = 256
XATTN_HEADS = 4
XATTN_HEAD_DIM = 128
TOP_K = 2
ROPE_BASE = 10000.0
EPS = 1e-6

LANES = 128
VMEM_LIMIT_BYTES = 56 * 1024 * 1024

BF16 = jnp.bfloat16
F32 = jnp.float32


def _params(n_axes):
    return pltpu.CompilerParams(
        dimension_semantics=("arbitrary",) * n_axes,
        vmem_limit_bytes=VMEM_LIMIT_BYTES,
    )


def _rms_scale(x):
    return lax.rsqrt(jnp.mean(x * x, axis=-1, keepdims=True) + EPS)


def _rmsnorm_kernel(x_ref, g_ref, o_ref):
    x = x_ref[...]
    o_ref[...] = (x * _rms_scale(x) * g_ref[...]).astype(o_ref.dtype)


def _rmsnorm(x, gain, out_dtype, tm=256):
    m, d = x.shape
    return pl.pallas_call(
        _rmsnorm_kernel,
        grid=(m // tm,),
        in_specs=[
            pl.BlockSpec((tm, d), lambda i: (i, 0)),
            pl.BlockSpec((1, d), lambda i: (0, 0)),
        ],
        out_specs=pl.BlockSpec((tm, d), lambda i: (i, 0)),
        out_shape=jax.ShapeDtypeStruct((m, d), out_dtype),
        compiler_params=_params(1),
        name="rmsnorm",
    )(x, gain.reshape(1, d))


def _matmul_kernel(a_ref, w_ref, *rest, has_residual):
    o_ref = rest[-1]
    acc = jnp.dot(a_ref[...], w_ref[...], preferred_element_type=F32)
    if has_residual:
        acc = acc + rest[0][...]
    o_ref[...] = acc.astype(o_ref.dtype)


def _matmul(a, w, *, tm, tn, out_dtype, residual=None, name="matmul"):
    m, k = a.shape
    n = w.shape[1]
    in_specs = [
        pl.BlockSpec((tm, k), lambda i, j: (i, 0)),
        pl.BlockSpec((k, tn), lambda i, j: (0, j)),
    ]
    args = [a, w]
    if residual is not None:
        in_specs.append(pl.BlockSpec((tm, tn), lambda i, j: (i, j)))
        args.append(residual)
    return pl.pallas_call(
        functools.partial(_matmul_kernel, has_residual=residual is not None),
        grid=(m // tm, n // tn),
        in_specs=in_specs,
        out_specs=pl.BlockSpec((tm, tn), lambda i, j: (i, j)),
        out_shape=jax.ShapeDtypeStruct((m, n), out_dtype),
        compiler_params=_params(2),
        name=name,
    )(*args)


def _gate_up_kernel(h_ref, wg_ref, wu_ref, *rest, blocks_per_expert):
    o_ref = rest[-1]
    h = h_ref[...]
    g = jnp.dot(h, wg_ref[...], preferred_element_type=F32)
    u = jnp.dot(h, wu_ref[...], preferred_element_type=F32)
    act = g * jax.nn.sigmoid(g) * u
    if blocks_per_expert:
        comb = rest[0][...]
        e = pl.program_id(1) // blocks_per_expert
        lane = lax.broadcasted_iota(jnp.int32, comb.shape, 1)
        act = act * jnp.sum(jnp.where(lane == e, comb, 0.0), axis=-1, keepdims=True)
    o_ref[...] = act.astype(o_ref.dtype)


def _gate_up(h, wg, wu, combine=None, *, tm=1024, tn=256):
    m, d = h.shape
    if wg.ndim == 3:
        n_exp, _, f = wg.shape
        bpe = f // tn
        w_spec = pl.BlockSpec((None, d, tn), lambda i, j: (j // bpe, 0, j % bpe))
        n = n_exp * f
    else:
        bpe = 0
        n = wg.shape[1]
        w_spec = pl.BlockSpec((d, tn), lambda i, j: (0, j))
    in_specs = [pl.BlockSpec((tm, d), lambda i, j: (i, 0)), w_spec, w_spec]
    args = [h, wg, wu]
    if bpe:
        in_specs.append(pl.BlockSpec((tm, LANES), lambda i, j: (i, 0)))
        args.append(combine)
    return pl.pallas_call(
        functools.partial(_gate_up_kernel, blocks_per_expert=bpe),
        grid=(m // tm, n // tn),
        in_specs=in_specs,
        out_specs=pl.BlockSpec((tm, tn), lambda i, j: (i, j)),
        out_shape=jax.ShapeDtypeStruct((m, n), BF16),
        compiler_params=_params(2),
        name="gate_up",
    )(*args)


def _rope_kernel(pos_ref, invf_ref, cos_ref, sin_ref):
    ang = pos_ref[...] * invf_ref[...]
    cos_ref[...] = jnp.cos(ang)
    sin_ref[...] = jnp.sin(ang)


def _rope_table(pos, tm=1024):
    t = pos.shape[0]
    half = RET_HEAD_DIM // 2
    inv_freq = (ROPE_BASE ** (-jnp.arange(half, dtype=F32) / half)).reshape(1, half)
    spec = pl.BlockSpec((tm, half), lambda i: (i, 0))
    return pl.pallas_call(
        _rope_kernel,
        grid=(t // tm,),
        in_specs=[pl.BlockSpec((tm, 1), lambda i: (i, 0)), pl.BlockSpec((1, half), lambda i: (0, 0))],
        out_specs=[spec, spec],
        out_shape=[jax.ShapeDtypeStruct((t, half), F32)] * 2,
        compiler_params=_params(1),
        name="rope_table",
    )(pos, inv_freq)


def _mixer_kernel(proj_ref, cos_ref, sin_ref, wsp_ref, bsp_ref, sgun_ref, retn_ref, o_ref, state_ref,
                  *, sgu_width, ret_width):
    sgu_heads = sgu_width // SGU_HEAD_DIM
    ret_heads = ret_width // RET_HEAD_DIM
    half = RET_HEAD_DIM // 2

    @pl.when(pl.program_id(1) == 0)
    def _():
        state_ref[...] = jnp.zeros_like(state_ref)

    row = lax.broadcasted_iota(jnp.int32, (CHUNK, CHUNK), 0)
    col = lax.broadcasted_iota(jnp.int32, (CHUNK, CHUNK), 1)
    causal = row >= col

    v = jax.nn.gelu(proj_ref[:, sgu_width:2 * sgu_width].astype(F32))
    v = (v * _rms_scale(v) * sgun_ref[...]).astype(BF16)
    for h in range(sgu_heads):
        sl = slice(h * SGU_HEAD_DIM, (h + 1) * SGU_HEAD_DIM)
        w = jnp.where(causal, wsp_ref[h], 0.0).astype(BF16)
        mixed = jnp.dot(w, v[:, sl], preferred_element_type=F32) + bsp_ref[:, h:h + 1]
        u = jax.nn.gelu(proj_ref[:, sl].astype(F32))
        o_ref[:, sl] = (u * mixed).astype(o_ref.dtype)

    cos = cos_ref[...]
    sin = sin_ref[...]
    rel = (row - col).astype(F32)
    pos_in_chunk = lax.broadcasted_iota(jnp.int32, (CHUNK, 1), 0).astype(F32)
    q0, k0, v0, g0 = (2 * sgu_width + i * ret_width for i in range(4))

    def rotary(base):
        t1 = proj_ref[:, base:base + half].astype(F32)
        t2 = proj_ref[:, base + half:base + RET_HEAD_DIM].astype(F32)
        return jnp.concatenate([t1 * cos - t2 * sin, t2 * cos + t1 * sin], axis=-1)

    for h in range(ret_heads):
        log_gamma = math.log1p(-(2.0 ** (-5.0 - h)))
        off = h * RET_HEAD_DIM
        q = rotary(q0 + off)
        k = rotary(k0 + off) * (RET_HEAD_DIM ** -0.5)
        vh = proj_ref[:, v0 + off:v0 + off + RET_HEAD_DIM]
        g = proj_ref[:, g0 + off:g0 + off + RET_HEAD_DIM].astype(F32)
        decay_intra = jnp.where(causal, jnp.exp(log_gamma * jnp.maximum(rel, 0.0)), 0.0)
        decay_key = jnp.exp(log_gamma * (CHUNK - 1 - pos_in_chunk))
        decay_query = jnp.exp(log_gamma * (pos_in_chunk + 1.0))
        decay_chunk = math.exp(log_gamma * CHUNK)

        qb = q.astype(BF16)
        scores = lax.dot_general(qb, k.astype(BF16), (((1,), (1,)), ((), ())),
                                 preferred_element_type=F32) * decay_intra
        intra = jnp.dot(scores.astype(BF16), vh, preferred_element_type=F32)
        state = state_ref[h]
        inter = jnp.dot(qb, state.astype(BF16), preferred_element_type=F32) * decay_query
        update = lax.dot_general((k * decay_key).astype(BF16), vh, (((0,), (0,)), ((), ())),
                                 preferred_element_type=F32)
        state_ref[h] = decay_chunk * state + update

        out = intra + inter
        out = out * _rms_scale(out) * retn_ref[:, off:off + RET_HEAD_DIM]
        o_ref[:, sgu_width + off:sgu_width + off + RET_HEAD_DIM] = (
            g * jax.nn.sigmoid(g) * out).astype(o_ref.dtype)


def _mixer(proj, cos, sin, w_spatial, b_spatial_t, sgu_norm, ret_norm, *, batch, seq):
    t, in_cols = proj.shape
    sgu_width = sgu_norm.shape[0]
    ret_width = ret_norm.shape[0]
    nc = seq // CHUNK
    half = RET_HEAD_DIM // 2
    sgu_heads = sgu_width // SGU_HEAD_DIM
    ret_heads = ret_width // RET_HEAD_DIM
    row_block = lambda b, c: (b * nc + c, 0)
    const2 = lambda b, c: (0, 0)
    return pl.pallas_call(
        functools.partial(_mixer_kernel, sgu_width=sgu_width, ret_width=ret_width),
        grid=(batch, nc),
        in_specs=[
            pl.BlockSpec((CHUNK, in_cols), row_block),
            pl.BlockSpec((CHUNK, half), row_block),
            pl.BlockSpec((CHUNK, half), row_block),
            pl.BlockSpec((sgu_heads, CHUNK, CHUNK), lambda b, c: (0, 0, 0)),
            pl.BlockSpec((CHUNK, sgu_heads), const2),
            pl.BlockSpec((1, sgu_width), const2),
            pl.BlockSpec((1, ret_width), const2),
        ],
        out_specs=pl.BlockSpec((CHUNK, sgu_width + ret_width), row_block),
        out_shape=jax.ShapeDtypeStruct((t, sgu_width + ret_width), BF16),
        scratch_shapes=[pltpu.VMEM((ret_heads, RET_HEAD_DIM, RET_HEAD_DIM), F32)],
        compiler_params=_params(2),
        name="mixer",
    )(proj, cos, sin, w_spatial, b_spatial_t, sgu_norm.reshape(1, -1), ret_norm.reshape(1, -1))


def _xattn_kernel(x_ref, gx_ref, wq_ref, kv_ref, wo_ref, gf_ref, *rest, n_experts):
    if n_experts:
        wr_ref, x_out, h_out, comb_out = rest
    else:
        x_out, h_out = rest
    width = XATTN_HEADS * XATTN_HEAD_DIM
    x = x_ref[...]
    xn = (x * _rms_scale(x) * gx_ref[...]).astype(BF16)
    q = jnp.dot(xn, wq_ref[...], preferred_element_type=F32) * (XATTN_HEAD_DIM ** -0.5)
    heads = []
    for h in range(XATTN_HEADS):
        sl = slice(h * XATTN_HEAD_DIM, (h + 1) * XATTN_HEAD_DIM)
        kh = kv_ref[:, sl]
        vh = kv_ref[:, width + h * XATTN_HEAD_DIM:width + (h + 1) * XATTN_HEAD_DIM]
        s = lax.dot_general(q[:, sl].astype(BF16), kh, (((1,), (1,)), ((), ())),
                            preferred_element_type=F32)
        p = jnp.exp(s - jnp.max(s, axis=-1, keepdims=True))
        denom = jnp.sum(p, axis=-1, keepdims=True)
        heads.append(jnp.dot(p.astype(BF16), vh, preferred_element_type=F32) / denom)
    o = jnp.concatenate(heads, axis=-1).astype(BF16)
    x_new = x + jnp.dot(o, wo_ref[...], preferred_element_type=F32)
    x_out[...] = x_new
    hn = x_new * _rms_scale(x_new) * gf_ref[...]
    h_out[...] = hn.astype(h_out.dtype)

    if n_experts:
        logits = jnp.dot(hn, wr_ref[...], preferred_element_type=F32, precision=lax.Precision.HIGHEST)
        lane = lax.broadcasted_iota(jnp.int32, logits.shape, 1).astype(F32)
        neg_inf = jnp.float32(-jnp.inf)
        lg = jnp.where(lane < n_experts, logits, neg_inf)
        m1 = jnp.max(lg, axis=-1, keepdims=True)
        i1 = jnp.min(jnp.where(lg == m1, lane, float(LANES)), axis=-1, keepdims=True)
        lg2 = jnp.where(lane == i1, neg_inf, lg)
        m2 = jnp.max(lg2, axis=-1, keepdims=True)
        i2 = jnp.min(jnp.where(lg2 == m2, lane, float(LANES)), axis=-1, keepdims=True)
        e2 = jnp.exp(m2 - m1)
        w1 = 1.0 / (1.0 + e2)
        w2 = e2 / (1.0 + e2)
        comb_out[...] = jnp.where(lane == i1, w1, 0.0) + jnp.where(lane == i2, w2, 0.0)


def _xattn(x, xattn_norm, w_cq, kv, w_co, ffn_norm, w_router=None, *, seq, mem_len, tm=256):
    t, d = x.shape
    width = w_cq.shape[1]
    tiles_per_batch = seq // tm
    row = lambda i: (i, 0)
    const = lambda i: (0, 0)
    in_specs = [
        pl.BlockSpec((tm, d), row),
        pl.BlockSpec((1, d), const),
        pl.BlockSpec((d, width), const),
        pl.BlockSpec((mem_len, 2 * width), lambda i: (i // tiles_per_batch, 0)),
        pl.BlockSpec((width, d), const),
        pl.BlockSpec((1, d), const),
    ]
    args = [x, xattn_norm.reshape(1, d), w_cq, kv, w_co, ffn_norm.reshape(1, d)]
    out_specs = [pl.BlockSpec((tm, d), row), pl.BlockSpec((tm, d), row)]
    out_shape = [jax.ShapeDtypeStruct((t, d), F32), jax.ShapeDtypeStruct((t, d), BF16)]
    n_experts = 0
    if w_router is not None:
        n_experts = w_router.shape[1]
        in_specs.append(pl.BlockSpec((d, LANES), const))
        args.append(jnp.pad(w_router, ((0, 0), (0, LANES - n_experts))))
        out_specs.append(pl.BlockSpec((tm, LANES), row))
        out_shape.append(jax.ShapeDtypeStruct((t, LANES), F32))
    return pl.pallas_call(
        functools.partial(_xattn_kernel, n_experts=n_experts),
        grid=(t // tm,),
        in_specs=in_specs,
        out_specs=out_specs,
        out_shape=out_shape,
        compiler_params=_params(1),
        name="xattn",
    )(*args)


def kernel(x, mem, positions, mix_norm, w_in, sgu_norm, w_spatial, b_spatial, ret_norm, w_out,
           xattn_norm, mem_norm, w_cq, w_ck, w_cv, w_co, ffn_norm, w_gate, w_up, w_down,
           w_router, we_gate, we_up, we_down, final_norm):
    batch, seq, d = x.shape
    mem_len = mem.shape[1]
    depth = w_in.shape[0]
    t = batch * seq

    xf = x.reshape(t, d)
    memf = mem.reshape(batch * mem_len, d)
    cos, sin = _rope_table(positions.astype(F32).reshape(t, 1))

    for layer in range(depth):
        h = _rmsnorm(xf, mix_norm[layer], BF16)
        proj = _matmul(h, w_in[layer].astype(BF16), tm=1024, tn=512, out_dtype=BF16, name="in_proj")
        mix = _mixer(proj, cos, sin, w_spatial[layer], b_spatial[layer].T, sgu_norm[layer],
                     ret_norm[layer], batch=batch, seq=seq)
        xf = _matmul(mix, w_out[layer].astype(BF16), tm=1024, tn=512, out_dtype=F32, residual=xf,
                     name="out_proj")

        mem_n = _rmsnorm(memf, mem_norm[layer], BF16)
        w_kv = jnp.concatenate([w_ck[layer], w_cv[layer]], axis=1).astype(BF16)
        kv = _matmul(mem_n, w_kv, tm=batch * mem_len, tn=512, out_dtype=BF16, name="mem_kv")

        moe = layer % 2 == 1
        idx = layer // 2
        outs = _xattn(xf, xattn_norm[layer], w_cq[layer].astype(BF16), kv, w_co[layer].astype(BF16),
                      ffn_norm[layer], w_router[idx] if moe else None, seq=seq, mem_len=mem_len)
        if moe:
            xf, hf, combine = outs
            n_exp, f, _ = we_down[idx].shape
            act = _gate_up(hf, we_gate[idx].astype(BF16), we_up[idx].astype(BF16), combine)
            wd = we_down[idx].reshape(n_exp * f, d).astype(BF16)
        else:
            xf, hf = outs
            act = _gate_up(hf, w_gate[idx].astype(BF16), w_up[idx].astype(BF16))
            wd = w_down[idx].astype(BF16)
        xf = _matmul(act, wd, tm=512, tn=256, out_dtype=F32, residual=xf, name="down_proj")

    return _rmsnorm(xf, final_norm, F32).reshape(batch, seq, d)
```

```python
import functools
import math

import jax
import jax.numpy as jnp
from jax import lax
from jax.experimental import pallas as pl
from jax.experimental.pallas import tpu as pltpu

CHUNK = 128
SGU_HEAD_DIM = 128
RET_HEAD_DIM = 256
XATTN_HEADS = 4
XATTN_HEAD_DIM = 128
TOP_K = 2
ROPE_BASE = 10000.0
EPS = 1e-6

LANES = 128
VMEM_LIMIT_BYTES = 56 * 1024 * 1024

BF16 = jnp.bfloat16
F32 = jnp.float32


def _params(n_axes):
    return pltpu.CompilerParams(
        dimension_semantics=("arbitrary",) * n_axes,
        vmem_limit_bytes=VMEM_LIMIT_BYTES,
    )


def _rms_scale(x):
    return lax.rsqrt(jnp.mean(x * x, axis=-1, keepdims=True) + EPS)


def _rmsnorm_kernel(x_ref, g_ref, o_ref):
    x = x_ref[...]
    o_ref[...] = (x * _rms_scale(x) * g_ref[...]).astype(o_ref.dtype)


def _rmsnorm(x, gain, out_dtype, tm=256):
    m, d = x.shape
    return pl.pallas_call(
        _rmsnorm_kernel,
        grid=(m // tm,),
        in_specs=[
            pl.BlockSpec((tm, d), lambda i: (i, 0)),
            pl.BlockSpec((1, d), lambda i: (0, 0)),
        ],
        out_specs=pl.BlockSpec((tm, d), lambda i: (i, 0)),
        out_shape=jax.ShapeDtypeStruct((m, d), out_dtype),
        compiler_params=_params(1),
        name="rmsnorm",
    )(x, gain.reshape(1, d))


def _matmul_kernel(a_ref, w_ref, *rest, has_residual):
    o_ref = rest[-1]
    acc = jnp.dot(a_ref[...], w_ref[...], preferred_element_type=F32)
    if has_residual:
        acc = acc + rest[0][...]
    o_ref[...] = acc.astype(o_ref.dtype)


def _matmul(a, w, *, tm, tn, out_dtype, residual=None, name="matmul"):
    m, k = a.shape
    n = w.shape[1]
    in_specs = [
        pl.BlockSpec((tm, k), lambda i, j: (i, 0)),
        pl.BlockSpec((k, tn), lambda i, j: (0, j)),
    ]
    args = [a, w]
    if residual is not None:
        in_specs.append(pl.BlockSpec((tm, tn), lambda i, j: (i, j)))
        args.append(residual)
    return pl.pallas_call(
        functools.partial(_matmul_kernel, has_residual=residual is not None),
        grid=(m // tm, n // tn),
        in_specs=in_specs,
        out_specs=pl.BlockSpec((tm, tn), lambda i, j: (i, j)),
        out_shape=jax.ShapeDtypeStruct((m, n), out_dtype),
        compiler_params=_params(2),
        name=name,
    )(*args)


def _matmul_wres_kernel(a_ref, w_ref, *rest, has_residual):
    o_ref, wb_ref = rest[-2], rest[-1]

    @pl.when(pl.program_id(1) == 0)
    def _():
        wb_ref[...] = w_ref[...].astype(BF16)

    acc = jnp.dot(a_ref[...], wb_ref[...], preferred_element_type=F32)
    if has_residual:
        acc = acc + rest[0][...]
    o_ref[...] = acc.astype(o_ref.dtype)


def _matmul_wres(a, w, layer, *, tm, tn, out_dtype, residual=None, name="matmul_wres"):
    m, k = a.shape
    n = w.shape[2]
    in_specs = [
        pl.BlockSpec((tm, k), lambda j, i: (i, 0)),
        pl.BlockSpec((None, k, tn), lambda j, i: (layer, 0, j)),
    ]
    args = [a, w]
    if residual is not None:
        in_specs.append(pl.BlockSpec((tm, tn), lambda j, i: (i, j)))
        args.append(residual)
    return pl.pallas_call(
        functools.partial(_matmul_wres_kernel, has_residual=residual is not None),
        grid=(n // tn, m // tm),
        in_specs=in_specs,
        out_specs=pl.BlockSpec((tm, tn), lambda j, i: (i, j)),
        out_shape=jax.ShapeDtypeStruct((m, n), out_dtype),
        scratch_shapes=[pltpu.VMEM((k, tn), BF16)],
        compiler_params=_params(2),
        name=name,
    )(*args)


def _gate_up_kernel(h_ref, wg_ref, wu_ref, *rest, blocks_per_expert):
    o_ref, wgb_ref, wub_ref = rest[-3], rest[-2], rest[-1]

    @pl.when(pl.program_id(1) == 0)
    def _():
        wgb_ref[...] = wg_ref[...].astype(BF16)
        wub_ref[...] = wu_ref[...].astype(BF16)

    h = h_ref[...]
    g = jnp.dot(h, wgb_ref[...], preferred_element_type=F32)
    u = jnp.dot(h, wub_ref[...], preferred_element_type=F32)
    act = g * jax.nn.sigmoid(g) * u
    if blocks_per_expert:
        comb = rest[0][...]
        e = pl.program_id(0) // blocks_per_expert
        lane = lax.broadcasted_iota(jnp.int32, comb.shape, 1)
        act = act * jnp.sum(jnp.where(lane == e, comb, 0.0), axis=-1, keepdims=True)
    o_ref[...] = act.astype(o_ref.dtype)


def _gate_up(h, wg, wu, idx, combine=None, *, tm=1024, tn=256):
    m, d = h.shape
    if wg.ndim == 4:
        _, n_exp, _, f = wg.shape
        bpe = f // tn
        w_spec = pl.BlockSpec((None, None, d, tn), lambda j, i: (idx, j // bpe, 0, j % bpe))
        n = n_exp * f
    else:
        bpe = 0
        n = wg.shape[2]
        w_spec = pl.BlockSpec((None, d, tn), lambda j, i: (idx, 0, j))
    in_specs = [pl.BlockSpec((tm, d), lambda j, i: (i, 0)), w_spec, w_spec]
    args = [h, wg, wu]
    if bpe:
        in_specs.append(pl.BlockSpec((tm, LANES), lambda j, i: (i, 0)))
        args.append(combine)
    return pl.pallas_call(
        functools.partial(_gate_up_kernel, blocks_per_expert=bpe),
        grid=(n // tn, m // tm),
        in_specs=in_specs,
        out_specs=pl.BlockSpec((tm, tn), lambda j, i: (i, j)),
        out_shape=jax.ShapeDtypeStruct((m, n), BF16),
        scratch_shapes=[pltpu.VMEM((d, tn), BF16), pltpu.VMEM((d, tn), BF16)],
        compiler_params=_params(2),
        name="gate_up",
    )(*args)


def _rope_kernel(pos_ref, invf_ref, cos_ref, sin_ref):
    ang = pos_ref[...] * invf_ref[...]
    cos_ref[...] = jnp.cos(ang)
    sin_ref[...] = jnp.sin(ang)


def _rope_table(pos, tm=1024):
    t = pos.shape[0]
    half = RET_HEAD_DIM // 2
    inv_freq = (ROPE_BASE ** (-jnp.arange(half, dtype=F32) / half)).reshape(1, half)
    spec = pl.BlockSpec((tm, half), lambda i: (i, 0))
    return pl.pallas_call(
        _rope_kernel,
        grid=(t // tm,),
        in_specs=[pl.BlockSpec((tm, 1), lambda i: (i, 0)), pl.BlockSpec((1, half), lambda i: (0, 0))],
        out_specs=[spec, spec],
        out_shape=[jax.ShapeDtypeStruct((t, half), F32)] * 2,
        compiler_params=_params(1),
        name="rope_table",
    )(pos, inv_freq)


def _mixer_kernel(proj_ref, cos_ref, sin_ref, wsp_ref, bsp_ref, sgun_ref, retn_ref, o_ref, state_ref,
                  *, sgu_width, ret_width):
    sgu_heads = sgu_width // SGU_HEAD_DIM
    ret_heads = ret_width // RET_HEAD_DIM
    half = RET_HEAD_DIM // 2

    @pl.when(pl.program_id(1) == 0)
    def _():
        state_ref[...] = jnp.zeros_like(state_ref)

    row = lax.broadcasted_iota(jnp.int32, (CHUNK, CHUNK), 0)
    col = lax.broadcasted_iota(jnp.int32, (CHUNK, CHUNK), 1)
    causal = row >= col

    v = jax.nn.gelu(proj_ref[:, sgu_width:2 * sgu_width].astype(F32))
    v = (v * _rms_scale(v) * sgun_ref[...]).astype(BF16)
    for h in range(sgu_heads):
        sl = slice(h * SGU_HEAD_DIM, (h + 1) * SGU_HEAD_DIM)
        w = jnp.where(causal, wsp_ref[h], 0.0).astype(BF16)
        mixed = jnp.dot(w, v[:, sl], preferred_element_type=F32) + bsp_ref[:, h:h + 1]
        u = jax.nn.gelu(proj_ref[:, sl].astype(F32))
        o_ref[:, sl] = (u * mixed).astype(o_ref.dtype)

    cos = cos_ref[...]
    sin = sin_ref[...]
    rel = (row - col).astype(F32)
    pos_in_chunk = lax.broadcasted_iota(jnp.int32, (CHUNK, 1), 0).astype(F32)
    q0, k0, v0, g0 = (2 * sgu_width + i * ret_width for i in range(4))

    def rotary(base):
        t1 = proj_ref[:, base:base + half].astype(F32)
        t2 = proj_ref[:, base + half:base + RET_HEAD_DIM].astype(F32)
        return jnp.concatenate([t1 * cos - t2 * sin, t2 * cos + t1 * sin], axis=-1)

    for h in range(ret_heads):
        log_gamma = math.log1p(-(2.0 ** (-5.0 - h)))
        off = h * RET_HEAD_DIM
        q = rotary(q0 + off)
        k = rotary(k0 + off) * (RET_HEAD_DIM ** -0.5)
        vh = proj_ref[:, v0 + off:v0 + off + RET_HEAD_DIM]
        g = proj_ref[:, g0 + off:g0 + off + RET_HEAD_DIM].astype(F32)
        decay_intra = jnp.where(causal, jnp.exp(log_gamma * jnp.maximum(rel, 0.0)), 0.0)
        decay_key = jnp.exp(log_gamma * (CHUNK - 1 - pos_in_chunk))
        decay_query = jnp.exp(log_gamma * (pos_in_chunk + 1.0))
        decay_chunk = math.exp(log_gamma * CHUNK)

        qb = q.astype(BF16)
        scores = lax.dot_general(qb, k.astype(BF16), (((1,), (1,)), ((), ())),
                                 preferred_element_type=F32) * decay_intra
        intra = jnp.dot(scores.astype(BF16), vh, preferred_element_type=F32)
        state = state_ref[h]
        inter = jnp.dot(qb, state.astype(BF16), preferred_element_type=F32) * decay_query
        update = lax.dot_general((k * decay_key).astype(BF16), vh, (((0,), (0,)), ((), ())),
                                 preferred_element_type=F32)
        state_ref[h] = decay_chunk * state + update

        out = intra + inter
        out = out * _rms_scale(out) * retn_ref[:, off:off + RET_HEAD_DIM]
        o_ref[:, sgu_width + off:sgu_width + off + RET_HEAD_DIM] = (
            g * jax.nn.sigmoid(g) * out).astype(o_ref.dtype)


def _mixer(proj, cos, sin, w_spatial, b_spatial_t, sgu_norm, ret_norm, *, batch, seq):
    t, in_cols = proj.shape
    sgu_width = sgu_norm.shape[0]
    ret_width = ret_norm.shape[0]
    nc = seq // CHUNK
    half = RET_HEAD_DIM // 2
    sgu_heads = sgu_width // SGU_HEAD_DIM
    ret_heads = ret_width // RET_HEAD_DIM
    row_block = lambda b, c: (b * nc + c, 0)
    const2 = lambda b, c: (0, 0)
    return pl.pallas_call(
        functools.partial(_mixer_kernel, sgu_width=sgu_width, ret_width=ret_width),
        grid=(batch, nc),
        in_specs=[
            pl.BlockSpec((CHUNK, in_cols), row_block),
            pl.BlockSpec((CHUNK, half), row_block),
            pl.BlockSpec((CHUNK, half), row_block),
            pl.BlockSpec((sgu_heads, CHUNK, CHUNK), lambda b, c: (0, 0, 0)),
            pl.BlockSpec((CHUNK, sgu_heads), const2),
            pl.BlockSpec((1, sgu_width), const2),
            pl.BlockSpec((1, ret_width), const2),
        ],
        out_specs=pl.BlockSpec((CHUNK, sgu_width + ret_width), row_block),
        out_shape=jax.ShapeDtypeStruct((t, sgu_width + ret_width), BF16),
        scratch_shapes=[pltpu.VMEM((ret_heads, RET_HEAD_DIM, RET_HEAD_DIM), F32)],
        compiler_params=_params(2),
        name="mixer",
    )(proj, cos, sin, w_spatial, b_spatial_t, sgu_norm.reshape(1, -1), ret_norm.reshape(1, -1))


def _xattn_kernel(x_ref, gx_ref, wq_ref, kv_ref, wo_ref, gf_ref, *rest, n_experts):
    if n_experts:
        wr_ref, x_out, h_out, comb_out = rest
    else:
        x_out, h_out = rest
    width = XATTN_HEADS * XATTN_HEAD_DIM
    x = x_ref[...]
    xn = (x * _rms_scale(x) * gx_ref[...]).astype(BF16)
    q = jnp.dot(xn, wq_ref[...], preferred_element_type=F32) * (XATTN_HEAD_DIM ** -0.5)
    heads = []
    for h in range(XATTN_HEADS):
        sl = slice(h * XATTN_HEAD_DIM, (h + 1) * XATTN_HEAD_DIM)
        kh = kv_ref[:, sl]
        vh = kv_ref[:, width + h * XATTN_HEAD_DIM:width + (h + 1) * XATTN_HEAD_DIM]
        s = lax.dot_general(q[:, sl].astype(BF16), kh, (((1,), (1,)), ((), ())),
                            preferred_element_type=F32)
        p = jnp.exp(s - jnp.max(s, axis=-1, keepdims=True))
        denom = jnp.sum(p, axis=-1, keepdims=True)
        heads.append(jnp.dot(p.astype(BF16), vh, preferred_element_type=F32) / denom)
    o = jnp.concatenate(heads, axis=-1).astype(BF16)
    x_new = x + jnp.dot(o, wo_ref[...], preferred_element_type=F32)
    x_out[...] = x_new
    hn = x_new * _rms_scale(x_new) * gf_ref[...]
    h_out[...] = hn.astype(h_out.dtype)

    if n_experts:
        logits = jnp.dot(hn, wr_ref[...], preferred_element_type=F32, precision=lax.Precision.HIGHEST)
        lane = lax.broadcasted_iota(jnp.int32, logits.shape, 1).astype(F32)
        neg_inf = jnp.float32(-jnp.inf)
        lg = jnp.where(lane < n_experts, logits, neg_inf)
        m1 = jnp.max(lg, axis=-1, keepdims=True)
        i1 = jnp.min(jnp.where(lg == m1, lane, float(LANES)), axis=-1, keepdims=True)
        lg2 = jnp.where(lane == i1, neg_inf, lg)
        m2 = jnp.max(lg2, axis=-1, keepdims=True)
        i2 = jnp.min(jnp.where(lg2 == m2, lane, float(LANES)), axis=-1, keepdims=True)
        e2 = jnp.exp(m2 - m1)
        w1 = 1.0 / (1.0 + e2)
        w2 = e2 / (1.0 + e2)
        comb_out[...] = jnp.where(lane == i1, w1, 0.0) + jnp.where(lane == i2, w2, 0.0)


def _xattn(x, xattn_norm, w_cq, kv, w_co, ffn_norm, w_router=None, *, seq, mem_len, tm=256):
    t, d = x.shape
    width = w_cq.shape[1]
    tiles_per_batch = seq // tm
    row = lambda i: (i, 0)
    const = lambda i: (0, 0)
    in_specs = [
        pl.BlockSpec((tm, d), row),
        pl.BlockSpec((1, d), const),
        pl.BlockSpec((d, width), const),
        pl.BlockSpec((mem_len, 2 * width), lambda i: (i // tiles_per_batch, 0)),
        pl.BlockSpec((width, d), const),
        pl.BlockSpec((1, d), const),
    ]
    args = [x, xattn_norm.reshape(1, d), w_cq, kv, w_co, ffn_norm.reshape(1, d)]
    out_specs = [pl.BlockSpec((tm, d), row), pl.BlockSpec((tm, d), row)]
    out_shape = [jax.ShapeDtypeStruct((t, d), F32), jax.ShapeDtypeStruct((t, d), BF16)]
    n_experts = 0
    if w_router is not None:
        n_experts = w_router.shape[1]
        in_specs.append(pl.BlockSpec((d, LANES), const))
        args.append(jnp.pad(w_router, ((0, 0), (0, LANES - n_experts))))
        out_specs.append(pl.BlockSpec((tm, LANES), row))
        out_shape.append(jax.ShapeDtypeStruct((t, LANES), F32))
    return pl.pallas_call(
        functools.partial(_xattn_kernel, n_experts=n_experts),
        grid=(t // tm,),
        in_specs=in_specs,
        out_specs=out_specs,
        out_shape=out_shape,
        compiler_params=_params(1),
        name="xattn",
    )(*args)


def kernel(x, mem, positions, mix_norm, w_in, sgu_norm, w_spatial, b_spatial, ret_norm, w_out,
           xattn_norm, mem_norm, w_cq, w_ck, w_cv, w_co, ffn_norm, w_gate, w_up, w_down,
           w_router, we_gate, we_up, we_down, final_norm):
    batch, seq, d = x.shape
    mem_len = mem.shape[1]
    depth = w_in.shape[0]
    t = batch * seq

    xf = x.reshape(t, d)
    memf = mem.reshape(batch * mem_len, d)
    cos, sin = _rope_table(positions.astype(F32).reshape(t, 1))

    for layer in range(depth):
        h = _rmsnorm(xf, mix_norm[layer], BF16)
        proj = _matmul_wres(h, w_in, layer, tm=1024, tn=512, out_dtype=BF16, name="in_proj")
        mix = _mixer(proj, cos, sin, w_spatial[layer], b_spatial[layer].T, sgu_norm[layer],
                     ret_norm[layer], batch=batch, seq=seq)
        xf = _matmul_wres(mix, w_out, layer, tm=1024, tn=512, out_dtype=F32, residual=xf,
                          name="out_proj")

        mem_n = _rmsnorm(memf, mem_norm[layer], BF16)
        w_kv = jnp.concatenate([w_ck[layer], w_cv[layer]], axis=1).astype(BF16)
        kv = _matmul(mem_n, w_kv, tm=batch * mem_len, tn=512, out_dtype=BF16, name="mem_kv")

        moe = layer % 2 == 1
        idx = layer // 2
        outs = _xattn(xf, xattn_norm[layer], w_cq[layer].astype(BF16), kv, w_co[layer].astype(BF16),
                      ffn_norm[layer], w_router[idx] if moe else None, seq=seq, mem_len=mem_len)
        if moe:
            xf, hf, combine = outs
            n_exp, f, _ = we_down[idx].shape
            act = _gate_up(hf, we_gate, we_up, idx, combine)
            wd = we_down[idx].reshape(n_exp * f, d).astype(BF16)
        else:
            xf, hf = outs
            act = _gate_up(hf, w_gate, w_up, idx)
            wd = w_down[idx].astype(BF16)
        xf = _matmul(act, wd, tm=512, tn=256, out_dtype=F32, residual=xf, name="down_proj")

    return _rmsnorm(xf, final_norm, F32).reshape(batch, seq, d)
```

```python
import functools
import math

import jax
import jax.numpy as jnp
from jax import lax
from jax.experimental import pallas as pl
from jax.experimental.pallas import tpu as pltpu

CHUNK = 128
SGU_HEAD_DIM = 128
RET_HEAD_DIM = 256
XATTN_HEADS = 4
XATTN_HEAD_DIM = 128
TOP_K = 2
ROPE_BASE = 10000.0
EPS = 1e-6

LANES = 128
VMEM_LIMIT_BYTES = 56 * 1024 * 1024

BF16 = jnp.bfloat16
F32 = jnp.float32


def _params(n_axes):
    return pltpu.CompilerParams(
        dimension_semantics=("arbitrary",) * n_axes,
        vmem_limit_bytes=VMEM_LIMIT_BYTES,
    )


def _rms_scale(x):
    return lax.rsqrt(jnp.mean(x * x, axis=-1, keepdims=True) + EPS)


def _rmsnorm_kernel(x_ref, g_ref, o_ref):
    x = x_ref[...]
    o_ref[...] = (x * _rms_scale(x) * g_ref[...]).astype(o_ref.dtype)


def _rmsnorm(x, gain, out_dtype, tm=256):
    m, d = x.shape
    return pl.pallas_call(
        _rmsnorm_kernel,
        grid=(m // tm,),
        in_specs=[
            pl.BlockSpec((tm, d), lambda i: (i, 0)),
            pl.BlockSpec((1, d), lambda i: (0, 0)),
        ],
        out_specs=pl.BlockSpec((tm, d), lambda i: (i, 0)),
        out_shape=jax.ShapeDtypeStruct((m, d), out_dtype),
        compiler_params=_params(1),
        name="rmsnorm",
    )(x, gain.reshape(1, d))


def _matmul_kernel(a_ref, w_ref, *rest, has_residual):
    o_ref = rest[-1]
    acc = jnp.dot(a_ref[...], w_ref[...], preferred_element_type=F32)
    if has_residual:
        acc = acc + rest[0][...]
    o_ref[...] = acc.astype(o_ref.dtype)


def _matmul(a, w, *, tm, tn, out_dtype, residual=None, name="matmul"):
    m, k = a.shape
    n = w.shape[1]
    in_specs = [
        pl.BlockSpec((tm, k), lambda i, j: (i, 0)),
        pl.BlockSpec((k, tn), lambda i, j: (0, j)),
    ]
    args = [a, w]
    if residual is not None:
        in_specs.append(pl.BlockSpec((tm, tn), lambda i, j: (i, j)))
        args.append(residual)
    return pl.pallas_call(
        functools.partial(_matmul_kernel, has_residual=residual is not None),
        grid=(m // tm, n // tn),
        in_specs=in_specs,
        out_specs=pl.BlockSpec((tm, tn), lambda i, j: (i, j)),
        out_shape=jax.ShapeDtypeStruct((m, n), out_dtype),
        compiler_params=_params(2),
        name=name,
    )(*args)


def _matmul_wres_kernel(a_ref, w_ref, *rest, has_residual):
    o_ref, wb_ref = rest[-2], rest[-1]

    @pl.when(pl.program_id(1) == 0)
    def _():
        wb_ref[...] = w_ref[...].astype(BF16)

    acc = jnp.dot(a_ref[...], wb_ref[...], preferred_element_type=F32)
    if has_residual:
        acc = acc + rest[0][...]
    o_ref[...] = acc.astype(o_ref.dtype)


def _matmul_wres(a, w, layer, *, tm, tn, out_dtype, residual=None, k_split=1, k_part=0,
                 name="matmul_wres"):
    m = a.shape[0]
    k = a.shape[1] // k_split
    n = w.shape[2]
    in_specs = [
        pl.BlockSpec((tm, k), lambda j, i: (i, k_part)),
        pl.BlockSpec((None, k, tn), lambda j, i: (layer, k_part, j)),
    ]
    args = [a, w]
    if residual is not None:
        in_specs.append(pl.BlockSpec((tm, tn), lambda j, i: (i, j)))
        args.append(residual)
    return pl.pallas_call(
        functools.partial(_matmul_wres_kernel, has_residual=residual is not None),
        grid=(n // tn, m // tm),
        in_specs=in_specs,
        out_specs=pl.BlockSpec((tm, tn), lambda j, i: (i, j)),
        out_shape=jax.ShapeDtypeStruct((m, n), out_dtype),
        scratch_shapes=[pltpu.VMEM((k, tn), BF16)],
        compiler_params=_params(2),
        name=name,
    )(*args)


def _gate_up_kernel(h_ref, wg_ref, wu_ref, *rest, blocks_per_expert):
    o_ref, wgb_ref, wub_ref = rest[-3], rest[-2], rest[-1]

    @pl.when(pl.program_id(1) == 0)
    def _():
        wgb_ref[...] = wg_ref[...].astype(BF16)
        wub_ref[...] = wu_ref[...].astype(BF16)

    h = h_ref[...]
    g = jnp.dot(h, wgb_ref[...], preferred_element_type=F32)
    u = jnp.dot(h, wub_ref[...], preferred_element_type=F32)
    act = g * jax.nn.sigmoid(g) * u
    if blocks_per_expert:
        comb = rest[0][...]
        e = pl.program_id(0) // blocks_per_expert
        lane = lax.broadcasted_iota(jnp.int32, comb.shape, 1)
        act = act * jnp.sum(jnp.where(lane == e, comb, 0.0), axis=-1, keepdims=True)
    o_ref[...] = act.astype(o_ref.dtype)


def _gate_up(h, wg, wu, idx, combine=None, *, tm=1024, tn=256):
    m, d = h.shape
    if wg.ndim == 4:
        _, n_exp, _, f = wg.shape
        bpe = f // tn
        w_spec = pl.BlockSpec((None, None, d, tn), lambda j, i: (idx, j // bpe, 0, j % bpe))
        n = n_exp * f
    else:
        bpe = 0
        n = wg.shape[2]
        w_spec = pl.BlockSpec((None, d, tn), lambda j, i: (idx, 0, j))
    in_specs = [pl.BlockSpec((tm, d), lambda j, i: (i, 0)), w_spec, w_spec]
    args = [h, wg, wu]
    if bpe:
        in_specs.append(pl.BlockSpec((tm, LANES), lambda j, i: (i, 0)))
        args.append(combine)
    return pl.pallas_call(
        functools.partial(_gate_up_kernel, blocks_per_expert=bpe),
        grid=(n // tn, m // tm),
        in_specs=in_specs,
        out_specs=pl.BlockSpec((tm, tn), lambda j, i: (i, j)),
        out_shape=jax.ShapeDtypeStruct((m, n), BF16),
        scratch_shapes=[pltpu.VMEM((d, tn), BF16), pltpu.VMEM((d, tn), BF16)],
        compiler_params=_params(2),
        name="gate_up",
    )(*args)


def _rope_kernel(pos_ref, invf_ref, cos_ref, sin_ref):
    ang = pos_ref[...] * invf_ref[...]
    cos_ref[...] = jnp.cos(ang)
    sin_ref[...] = jnp.sin(ang)


def _rope_table(pos, tm=1024):
    t = pos.shape[0]
    half = RET_HEAD_DIM // 2
    inv_freq = (ROPE_BASE ** (-jnp.arange(half, dtype=F32) / half)).reshape(1, half)
    spec = pl.BlockSpec((tm, half), lambda i: (i, 0))
    return pl.pallas_call(
        _rope_kernel,
        grid=(t // tm,),
        in_specs=[pl.BlockSpec((tm, 1), lambda i: (i, 0)), pl.BlockSpec((1, half), lambda i: (0, 0))],
        out_specs=[spec, spec],
        out_shape=[jax.ShapeDtypeStruct((t, half), F32)] * 2,
        compiler_params=_params(1),
        name="rope_table",
    )(pos, inv_freq)


def _mixer_kernel(proj_ref, cos_ref, sin_ref, wsp_ref, bsp_ref, sgun_ref, retn_ref, o_ref, state_ref,
                  *, sgu_width, ret_width):
    sgu_heads = sgu_width // SGU_HEAD_DIM
    ret_heads = ret_width // RET_HEAD_DIM
    half = RET_HEAD_DIM // 2

    @pl.when(pl.program_id(1) == 0)
    def _():
        state_ref[...] = jnp.zeros_like(state_ref)

    row = lax.broadcasted_iota(jnp.int32, (CHUNK, CHUNK), 0)
    col = lax.broadcasted_iota(jnp.int32, (CHUNK, CHUNK), 1)
    causal = row >= col

    v = jax.nn.gelu(proj_ref[:, sgu_width:2 * sgu_width].astype(F32))
    v = (v * _rms_scale(v) * sgun_ref[...]).astype(BF16)
    for h in range(sgu_heads):
        sl = slice(h * SGU_HEAD_DIM, (h + 1) * SGU_HEAD_DIM)
        w = jnp.where(causal, wsp_ref[h], 0.0).astype(BF16)
        mixed = jnp.dot(w, v[:, sl], preferred_element_type=F32) + bsp_ref[:, h:h + 1]
        u = jax.nn.gelu(proj_ref[:, sl].astype(F32))
        o_ref[:, sl] = (u * mixed).astype(o_ref.dtype)

    cos = cos_ref[...]
    sin = sin_ref[...]
    rel = (row - col).astype(F32)
    pos_in_chunk = lax.broadcasted_iota(jnp.int32, (CHUNK, 1), 0).astype(F32)
    q0, k0, v0, g0 = (2 * sgu_width + i * ret_width for i in range(4))

    def rotary(base):
        t1 = proj_ref[:, base:base + half].astype(F32)
        t2 = proj_ref[:, base + half:base + RET_HEAD_DIM].astype(F32)
        return jnp.concatenate([t1 * cos - t2 * sin, t2 * cos + t1 * sin], axis=-1)

    for h in range(ret_heads):
        log_gamma = math.log1p(-(2.0 ** (-5.0 - h)))
        off = h * RET_HEAD_DIM
        q = rotary(q0 + off)
        k = rotary(k0 + off) * (RET_HEAD_DIM ** -0.5)
        vh = proj_ref[:, v0 + off:v0 + off + RET_HEAD_DIM]
        g = proj_ref[:, g0 + off:g0 + off + RET_HEAD_DIM].astype(F32)
        decay_intra = jnp.where(causal, jnp.exp(log_gamma * jnp.maximum(rel, 0.0)), 0.0)
        decay_key = jnp.exp(log_gamma * (CHUNK - 1 - pos_in_chunk))
        decay_query = jnp.exp(log_gamma * (pos_in_chunk + 1.0))
        decay_chunk = math.exp(log_gamma * CHUNK)

        qb = q.astype(BF16)
        scores = lax.dot_general(qb, k.astype(BF16), (((1,), (1,)), ((), ())),
                                 preferred_element_type=F32) * decay_intra
        intra = jnp.dot(scores.astype(BF16), vh, preferred_element_type=F32)
        state = state_ref[h]
        inter = jnp.dot(qb, state.astype(BF16), preferred_element_type=F32) * decay_query
        update = lax.dot_general((k * decay_key).astype(BF16), vh, (((0,), (0,)), ((), ())),
                                 preferred_element_type=F32)
        state_ref[h] = decay_chunk * state + update

        out = intra + inter
        out = out * _rms_scale(out) * retn_ref[:, off:off + RET_HEAD_DIM]
        o_ref[:, sgu_width + off:sgu_width + off + RET_HEAD_DIM] = (
            g * jax.nn.sigmoid(g) * out).astype(o_ref.dtype)


def _mixer(proj, cos, sin, w_spatial, b_spatial_t, sgu_norm, ret_norm, *, batch, seq):
    t, in_cols = proj.shape
    sgu_width = sgu_norm.shape[0]
    ret_width = ret_norm.shape[0]
    nc = seq // CHUNK
    half = RET_HEAD_DIM // 2
    sgu_heads = sgu_width // SGU_HEAD_DIM
    ret_heads = ret_width // RET_HEAD_DIM
    row_block = lambda b, c: (b * nc + c, 0)
    const2 = lambda b, c: (0, 0)
    return pl.pallas_call(
        functools.partial(_mixer_kernel, sgu_width=sgu_width, ret_width=ret_width),
        grid=(batch, nc),
        in_specs=[
            pl.BlockSpec((CHUNK, in_cols), row_block),
            pl.BlockSpec((CHUNK, half), row_block),
            pl.BlockSpec((CHUNK, half), row_block),
            pl.BlockSpec((sgu_heads, CHUNK, CHUNK), lambda b, c: (0, 0, 0)),
            pl.BlockSpec((CHUNK, sgu_heads), const2),
            pl.BlockSpec((1, sgu_width), const2),
            pl.BlockSpec((1, ret_width), const2),
        ],
        out_specs=pl.BlockSpec((CHUNK, sgu_width + ret_width), row_block),
        out_shape=jax.ShapeDtypeStruct((t, sgu_width + ret_width), BF16),
        scratch_shapes=[pltpu.VMEM((ret_heads, RET_HEAD_DIM, RET_HEAD_DIM), F32)],
        compiler_params=_params(2),
        name="mixer",
    )(proj, cos, sin, w_spatial, b_spatial_t, sgu_norm.reshape(1, -1), ret_norm.reshape(1, -1))


def _xattn_kernel(x_ref, gx_ref, wq_ref, kv_ref, wo_ref, gf_ref, *rest, n_experts):
    if n_experts:
        wr_ref, x_out, h_out, comb_out = rest
    else:
        x_out, h_out = rest
    width = XATTN_HEADS * XATTN_HEAD_DIM
    x = x_ref[...]
    xn = (x * _rms_scale(x) * gx_ref[...]).astype(BF16)
    q = jnp.dot(xn, wq_ref[...], preferred_element_type=F32) * (XATTN_HEAD_DIM ** -0.5)
    heads = []
    for h in range(XATTN_HEADS):
        sl = slice(h * XATTN_HEAD_DIM, (h + 1) * XATTN_HEAD_DIM)
        kh = kv_ref[:, sl]
        vh = kv_ref[:, width + h * XATTN_HEAD_DIM:width + (h + 1) * XATTN_HEAD_DIM]
        s = lax.dot_general(q[:, sl].astype(BF16), kh, (((1,), (1,)), ((), ())),
                            preferred_element_type=F32)
        p = jnp.exp(s - jnp.max(s, axis=-1, keepdims=True))
        denom = jnp.sum(p, axis=-1, keepdims=True)
        heads.append(jnp.dot(p.astype(BF16), vh, preferred_element_type=F32) / denom)
    o = jnp.concatenate(heads, axis=-1).astype(BF16)
    x_new = x + jnp.dot(o, wo_ref[...], preferred_element_type=F32)
    x_out[...] = x_new
    hn = x_new * _rms_scale(x_new) * gf_ref[...]
    h_out[...] = hn.astype(h_out.dtype)

    if n_experts:
        logits = jnp.dot(hn, wr_ref[...], preferred_element_type=F32, precision=lax.Precision.HIGHEST)
        lane = lax.broadcasted_iota(jnp.int32, logits.shape, 1).astype(F32)
        neg_inf = jnp.float32(-jnp.inf)
        lg = jnp.where(lane < n_experts, logits, neg_inf)
        m1 = jnp.max(lg, axis=-1, keepdims=True)
        i1 = jnp.min(jnp.where(lg == m1, lane, float(LANES)), axis=-1, keepdims=True)
        lg2 = jnp.where(lane == i1, neg_inf, lg)
        m2 = jnp.max(lg2, axis=-1, keepdims=True)
        i2 = jnp.min(jnp.where(lg2 == m2, lane, float(LANES)), axis=-1, keepdims=True)
        e2 = jnp.exp(m2 - m1)
        w1 = 1.0 / (1.0 + e2)
        w2 = e2 / (1.0 + e2)
        comb_out[...] = jnp.where(lane == i1, w1, 0.0) + jnp.where(lane == i2, w2, 0.0)


def _xattn(x, xattn_norm, w_cq, kv, w_co, ffn_norm, w_router=None, *, seq, mem_len, tm=256):
    t, d = x.shape
    width = w_cq.shape[1]
    tiles_per_batch = seq // tm
    row = lambda i: (i, 0)
    const = lambda i: (0, 0)
    in_specs = [
        pl.BlockSpec((tm, d), row),
        pl.BlockSpec((1, d), const),
        pl.BlockSpec((d, width), const),
        pl.BlockSpec((mem_len, 2 * width), lambda i: (i // tiles_per_batch, 0)),
        pl.BlockSpec((width, d), const),
        pl.BlockSpec((1, d), const),
    ]
    args = [x, xattn_norm.reshape(1, d), w_cq, kv, w_co, ffn_norm.reshape(1, d)]
    out_specs = [pl.BlockSpec((tm, d), row), pl.BlockSpec((tm, d), row)]
    out_shape = [jax.ShapeDtypeStruct((t, d), F32), jax.ShapeDtypeStruct((t, d), BF16)]
    n_experts = 0
    if w_router is not None:
        n_experts = w_router.shape[1]
        in_specs.append(pl.BlockSpec((d, LANES), const))
        args.append(jnp.pad(w_router, ((0, 0), (0, LANES - n_experts))))
        out_specs.append(pl.BlockSpec((tm, LANES), row))
        out_shape.append(jax.ShapeDtypeStruct((t, LANES), F32))
    return pl.pallas_call(
        functools.partial(_xattn_kernel, n_experts=n_experts),
        grid=(t // tm,),
        in_specs=in_specs,
        out_specs=out_specs,
        out_shape=out_shape,
        compiler_params=_params(1),
        name="xattn",
    )(*args)


def kernel(x, mem, positions, mix_norm, w_in, sgu_norm, w_spatial, b_spatial, ret_norm, w_out,
           xattn_norm, mem_norm, w_cq, w_ck, w_cv, w_co, ffn_norm, w_gate, w_up, w_down,
           w_router, we_gate, we_up, we_down, final_norm):
    batch, seq, d = x.shape
    mem_len = mem.shape[1]
    depth = w_in.shape[0]
    t = batch * seq

    xf = x.reshape(t, d)
    memf = mem.reshape(batch * mem_len, d)
    cos, sin = _rope_table(positions.astype(F32).reshape(t, 1))

    for layer in range(depth):
        h = _rmsnorm(xf, mix_norm[layer], BF16)
        proj = _matmul_wres(h, w_in, layer, tm=1024, tn=512, out_dtype=BF16, name="in_proj")
        mix = _mixer(proj, cos, sin, w_spatial[layer], b_spatial[layer].T, sgu_norm[layer],
                     ret_norm[layer], batch=batch, seq=seq)
        xf = _matmul_wres(mix, w_out, layer, tm=1024, tn=512, out_dtype=F32, residual=xf,
                          name="out_proj")

        mem_n = _rmsnorm(memf, mem_norm[layer], BF16)
        w_kv = jnp.concatenate([w_ck[layer], w_cv[layer]], axis=1).astype(BF16)
        kv = _matmul(mem_n, w_kv, tm=batch * mem_len, tn=512, out_dtype=BF16, name="mem_kv")

        moe = layer % 2 == 1
        idx = layer // 2
        outs = _xattn(xf, xattn_norm[layer], w_cq[layer].astype(BF16), kv, w_co[layer].astype(BF16),
                      ffn_norm[layer], w_router[idx] if moe else None, seq=seq, mem_len=mem_len)
        if moe:
            xf, hf, combine = outs
            n_exp, f, _ = we_down[idx].shape
            act = _gate_up(hf, we_gate, we_up, idx, combine)
            wd = we_down[idx].reshape(n_exp * f, d).astype(BF16)
        else:
            xf, hf = outs
            act = _gate_up(hf, w_gate, w_up, idx)
            for part in range(2):
                xf = _matmul_wres(act, w_down, idx, tm=512, tn=512, out_dtype=F32, residual=xf,
                                  k_split=2, k_part=part, name="down_proj_dense")
            continue
        xf = _matmul(act, wd, tm=512, tn=256, out_dtype=F32, residual=xf, name="down_proj")

    return _rmsnorm(xf, final_norm, F32).reshape(batch, seq, d)
```
